```python
import math
import jax, jax.numpy as jnp
from jax import lax
import numpy as np

D_MODEL = 1024
BATCH = 16
SEQ = 2048
DEPTH = 4
DEC_BATCH = 4
DEC_SEQ = 8192
PAST_LEN = 128

N_MIXERS = 2
N_ATTN_LAYERS = (DEPTH + 1) // 2
N_RET_LAYERS = DEPTH // 2
PLE_DIM = 256
D_FF = 4 * D_MODEL
A_HEAD_DIM = 64
A_HEADS = D_MODEL // (2 * A_HEAD_DIM)
A_QK = A_HEADS * 2 * A_HEAD_DIM
A_V = A_HEADS * 2 * A_HEAD_DIM
ROPE_THETA = 10000.0
Q_BLOCK = 128
R_HEADS = 4
R_DK = D_MODEL // R_HEADS
R_DV = 2 * D_MODEL // R_HEADS
R_QK = R_HEADS * R_DK
R_V = R_HEADS * R_DV
CHUNK = 128
XPOS_BASE = 10000.0

kernel_name = "hybrid_diffattn_retnet_encoder"


def rms_norm(x, w, eps=1e-6):
    xf = x.astype(jnp.float32)
    y = xf * lax.rsqrt(jnp.mean(xf * xf, axis=-1, keepdims=True) + eps)
    if w is not None:
        y = y * w.astype(jnp.float32)
    return y.astype(x.dtype)


def rope_tables(S, dim, dtype):
    inv_freq = 1.0 / (ROPE_THETA ** (jnp.arange(0, dim, 2, dtype=jnp.float32) / dim))
    ang = jnp.arange(S, dtype=jnp.float32)[:, None] * inv_freq[None, :]
    ang = jnp.concatenate([ang, ang], axis=-1)
    return jnp.cos(ang).astype(dtype), jnp.sin(ang).astype(dtype)


def apply_rope_half(x, cos, sin):
    x1, x2 = jnp.split(x, 2, axis=-1)
    return x * cos + jnp.concatenate([-x2, x1], axis=-1) * sin


def xpos_tables(S, dim, dtype):
    angle = 1.0 / (XPOS_BASE ** jnp.linspace(0.0, 1.0, dim // 2, dtype=jnp.float32))
    angle = jnp.repeat(angle, 2)
    ang = jnp.arange(S, dtype=jnp.float32)[:, None] * angle[None, :]
    return jnp.cos(ang).astype(dtype), jnp.sin(ang).astype(dtype)


def theta_shift(x, cos, sin):
    x1 = x[..., 0::2]
    x2 = x[..., 1::2]
    rot = jnp.stack([-x2, x1], axis=-1).reshape(x.shape)
    return x * cos + rot * sin


def diff_attention(h, w_qkv, w_o, lq1, lk1, lq2, lk2, subln, layer_idx):
    B, S, _ = h.shape
    dt = h.dtype
    qkv = h @ w_qkv
    q, k, v = jnp.split(qkv, [A_QK, 2 * A_QK], axis=-1)
    q = q.reshape(B, S, A_HEADS, 2, A_HEAD_DIM)
    k = k.reshape(B, S, A_HEADS, 2, A_HEAD_DIM)
    v = v.reshape(B, S, A_HEADS, 2 * A_HEAD_DIM)
    cos, sin = rope_tables(S, A_HEAD_DIM, dt)
    cos = cos[:, None, None, :]
    sin = sin[:, None, None, :]
    q = apply_rope_half(q, cos, sin) * (A_HEAD_DIM ** -0.5)
    k = apply_rope_half(k, cos, sin)
    q = q.transpose(0, 2, 3, 1, 4)
    k = k.transpose(0, 2, 3, 1, 4)
    v = v.transpose(0, 2, 1, 3)
    lam_init = 0.8 - 0.6 * math.exp(-0.3 * layer_idx)
    lam = (jnp.exp(jnp.sum(lq1.astype(jnp.float32) * lk1.astype(jnp.float32)))
           - jnp.exp(jnp.sum(lq2.astype(jnp.float32) * lk2.astype(jnp.float32)))
           + lam_init)
    nb = S // Q_BLOCK
    qb = q.reshape(B, A_HEADS, 2, nb, Q_BLOCK, A_HEAD_DIM).transpose(3, 0, 1, 2, 4, 5)

    def block(qi):
        s = jnp.einsum('bhcqd,bhckd->bhcqk', qi, k).astype(jnp.float32)
        pr = jax.nn.softmax(s, axis=-1)
        a = (pr[:, :, 0] - lam * pr[:, :, 1]).astype(v.dtype)
        return jnp.einsum('bhqk,bhkv->bhqv', a, v)

    o = lax.map(block, qb)
    o = o.transpose(1, 0, 3, 2, 4).reshape(B, S, A_HEADS, 2 * A_HEAD_DIM)
    o = rms_norm(o, subln, 1e-5) * (1.0 - lam_init)
    return o.reshape(B, S, A_V) @ w_o


def retention_chunked(q, k, v, lg, inclusive):
    B, H, S, dk = q.shape
    dv = v.shape[-1]
    dt = q.dtype
    n = S // CHUNK
    pos = jnp.arange(CHUNK, dtype=jnp.float32)
    diff = pos[:, None] - pos[None, :]
    mask = (diff >= 0) if inclusive else (diff > 0)
    dmask = jnp.exp(jnp.where(mask[None], diff[None] * lg[:, None, None], -jnp.inf)).astype(dt)
    xi = jnp.exp((pos + 1.0)[None, :] * lg[:, None]).astype(dt)[..., None]
    zeta = jnp.exp((CHUNK - 1.0 - pos)[None, :] * lg[:, None]).astype(dt)[..., None]
    cdecay = jnp.exp(CHUNK * lg).astype(dt)[:, None, None]

    def to_chunks(t):
        return t.reshape(B, H, n, CHUNK, t.shape[-1]).transpose(2, 0, 1, 3, 4)

    def step(R, inp):
        qc, kc, vc = inp
        inner = jnp.einsum('bhqk,bhkv->bhqv', jnp.einsum('bhqd,bhkd->bhqk', qc, kc) * dmask, vc)
        cross = jnp.einsum('bhqd,bhdv->bhqv', qc, R) * xi
        R_new = R * cdecay + jnp.einsum('bhkd,bhkv->bhdv', kc * zeta, vc)
        return R_new, inner + cross

    R0 = jnp.zeros((B, H, dk, dv), dt)
    _, out = lax.scan(step, R0, (to_chunks(q), to_chunks(k), to_chunks(v)))
    return out.transpose(1, 2, 0, 3, 4).reshape(B, H, S, dv)


def retention(h, w_in, w_out, decay):
    B, S, _ = h.shape
    dt = h.dtype
    qkvg = h @ w_in
    q, k, v, g = jnp.split(qkvg, [R_QK, 2 * R_QK, 2 * R_QK + R_V], axis=-1)
    q = q.reshape(B, S, R_HEADS, R_DK)
    k = k.reshape(B, S, R_HEADS, R_DK)
    v = v.reshape(B, S, R_HEADS, R_DV)
    cos, sin = xpos_tables(S, R_DK, dt)
    cos = cos[:, None, :]
    sin = sin[:, None, :]
    q = theta_shift(q, cos, sin).transpose(0, 2, 1, 3)
    k = (theta_shift(k, cos, sin) * (R_DK ** -0.5)).transpose(0, 2, 1, 3)
    v = v.transpose(0, 2, 1, 3)
    lg = -jnp.exp(decay.astype(jnp.float32))
    fwd = retention_chunked(q, k, v, lg[0], True)
    bwd = jnp.flip(retention_chunked(jnp.flip(q, 2), jnp.flip(k, 2), jnp.flip(v, 2), lg[1], False), 2)
    o = (fwd + bwd).transpose(0, 2, 1, 3)
    o = rms_norm(o, None, 1e-6).reshape(B, S, R_V)
    return (jax.nn.silu(g) * o) @ w_out


def encoder_trunk(x, p, norm_pre_mix, norm_post_mix, norm_pre_mlp, norm_post_mlp,
                  attn_w_qkv, attn_w_o, attn_lambda_q1, attn_lambda_k1, attn_lambda_q2,
                  attn_lambda_k2, attn_subln, ret_w_in, ret_w_out, ret_decay,
                  mlp_w_in, mlp_w_out, ple_w_proj, ple_w_gate):
    for i in range(DEPTH):
        j = i // N_MIXERS
        h = rms_norm(x, norm_pre_mix[i])
        if i % N_MIXERS == 0:
            m = diff_attention(h, attn_w_qkv[j], attn_w_o[j], attn_lambda_q1[j], attn_lambda_k1[j],
                               attn_lambda_q2[j], attn_lambda_k2[j], attn_subln[j], i)
        else:
            m = retention(h, ret_w_in[j], ret_w_out[j], ret_decay[j])
        x = x + rms_norm(m, norm_post_mix[i])
        h = rms_norm(x, norm_pre_mlp[i])
        m = jnp.square(jax.nn.relu(h @ mlp_w_in[i])) @ mlp_w_out[i]
        x = x + rms_norm(m, norm_post_mlp[i])
        gate = jax.nn.sigmoid(rms_norm(x, None) @ ple_w_gate[i])
        x = x + (p[i] @ ple_w_proj[i]) * gate
    return x


def setup_inputs(seed: int = 0) -> dict:
    key = jax.random.key(seed)
    ks = jax.random.split(key, 24)
    f32 = jnp.float32

    def nrm(k, shape, scale):
        return jax.random.normal(k, shape, f32) * scale

    def gain(k, shape):
        return 1.0 + 0.01 * jax.random.normal(k, shape, f32)

    base = np.log(-np.log(1.0 - 2.0 ** (-5.0 - np.arange(R_HEADS)))).astype(np.float32)
    ret_decay = jnp.asarray(base)[None, None, :] + 0.05 * jax.random.normal(ks[15], (N_RET_LAYERS, 2, R_HEADS), f32)
    return {
        "x_prompt": nrm(ks[0], (BATCH, SEQ, D_MODEL), 1.0),
        "x_sample": nrm(ks[1], (DEC_BATCH, DEC_SEQ, D_MODEL), 1.0),
        "p_prompt": nrm(ks[2], (DEPTH, BATCH, SEQ, PLE_DIM), 1.0),
        "p_sample": nrm(ks[3], (DEPTH, DEC_BATCH, DEC_SEQ, PLE_DIM), 1.0),
        "norm_pre_mix": gain(ks[4], (DEPTH, D_MODEL)),
        "norm_post_mix": gain(ks[5], (DEPTH, D_MODEL)),
        "norm_pre_mlp": gain(ks[6], (DEPTH, D_MODEL)),
        "norm_post_mlp": gain(ks[7], (DEPTH, D_MODEL)),
        "attn_w_qkv": nrm(ks[8], (N_ATTN_LAYERS, D_MODEL, 2 * A_QK + A_V), D_MODEL ** -0.5),
        "attn_w_o": nrm(ks[9], (N_ATTN_LAYERS, A_V, D_MODEL), A_V ** -0.5),
        "attn_lambda_q1": nrm(ks[10], (N_ATTN_LAYERS, A_HEAD_DIM), 0.1),
        "attn_lambda_k1": nrm(ks[11], (N_ATTN_LAYERS, A_HEAD_DIM), 0.1),
        "attn_lambda_q2": nrm(ks[12], (N_ATTN_LAYERS, A_HEAD_DIM), 0.1),
        "attn_lambda_k2": nrm(ks[13], (N_ATTN_LAYERS, A_HEAD_DIM), 0.1),
        "attn_subln": gain(ks[14], (N_ATTN_LAYERS, 2 * A_HEAD_DIM)),
        "ret_w_in": nrm(ks[16], (N_RET_LAYERS, D_MODEL, 2 * R_QK + 2 * R_V), D_MODEL ** -0.5),
        "ret_w_out": nrm(ks[17], (N_RET_LAYERS, R_V, D_MODEL), R_V ** -0.5),
        "ret_decay": ret_decay,
        "mlp_w_in": nrm(ks[18], (DEPTH, D_MODEL, D_FF), D_MODEL ** -0.5),
        "mlp_w_out": nrm(ks[19], (DEPTH, D_FF, D_MODEL), D_FF ** -0.5),
        "ple_w_proj": nrm(ks[20], (DEPTH, PLE_DIM, D_MODEL), PLE_DIM ** -0.5),
        "ple_w_gate": nrm(ks[21], (DEPTH, D_MODEL, D_MODEL), D_MODEL ** -0.5),
    }


def reference(x_prompt, x_sample, p_prompt, p_sample, norm_pre_mix, norm_post_mix,
              norm_pre_mlp, norm_post_mlp, attn_w_qkv, attn_w_o, attn_lambda_q1,
              attn_lambda_k1, attn_lambda_q2, attn_lambda_k2, attn_subln, ret_w_in,
              ret_w_out, ret_decay, mlp_w_in, mlp_w_out, ple_w_proj, ple_w_gate):
    y_prompt = encoder_trunk(x_prompt, p_prompt, norm_pre_mix, norm_post_mix, norm_pre_mlp,
                             norm_post_mlp, attn_w_qkv, attn_w_o, attn_lambda_q1, attn_lambda_k1,
                             attn_lambda_q2, attn_lambda_k2, attn_subln, ret_w_in, ret_w_out,
                             ret_decay, mlp_w_in, mlp_w_out, ple_w_proj, ple_w_gate)
    y_sample = encoder_trunk(x_sample, p_sample, norm_pre_mix, norm_post_mix, norm_pre_mlp,
                             norm_post_mlp, attn_w_qkv, attn_w_o, attn_lambda_q1, attn_lambda_k1,
                             attn_lambda_q2, attn_lambda_k2, attn_subln, ret_w_in, ret_w_out,
                             ret_decay, mlp_w_in, mlp_w_out, ple_w_proj, ple_w_gate)
    return (y_prompt, y_sample)
```

```python
import functools
import math

import jax
import jax.numpy as jnp
from jax import lax
from jax.experimental import pallas as pl
from jax.experimental.pallas import tpu as pltpu

F32 = jnp.float32
BF16 = jnp.bfloat16

LANES = 128
V7X_VMEM_BYTES = 64 * 2 ** 20
VMEM_LIMIT = V7X_VMEM_BYTES * 7 // 8

N_MIXERS = 2
A_HEAD_DIM = 64
R_HEADS = 4
ROPE_THETA = 10000.0
XPOS_BASE = 10000.0
NORM_EPS = 1e-6
SUBLN_EPS = 1e-5


def _params(*semantics):
    return pltpu.CompilerParams(dimension_semantics=semantics, vmem_limit_bytes=VMEM_LIMIT)


def _resident(shape):
    nd = len(shape)
    return pl.BlockSpec(shape, lambda *_: (0,) * nd, pipeline_mode=pl.Buffered(1))


def _rms(x, eps):
    return x * lax.rsqrt(jnp.mean(x * x, axis=-1, keepdims=True) + eps)


def _sigmoid(z):
    return 1.0 / (1.0 + jnp.exp(-z))


def _rotate(t, cos, sin_signed, use_up, shift):
    up = pltpu.roll(t, LANES - shift, 1)
    down = pltpu.roll(t, shift, 1)
    return t * cos + jnp.where(use_up, up, down) * sin_signed


def _attn_proj_kernel(x_ref, nw_ref, w_ref, cos_ref, sin_ref, q_ref, k_ref, v_ref, *, d_qk, d_v):
    h = (_rms(x_ref[...], NORM_EPS) * nw_ref[...]).astype(BF16)
    cos = cos_ref[...]
    sin = sin_ref[...]
    lane = lax.broadcasted_iota(jnp.int32, cos.shape, 1)
    use_up = (lane % A_HEAD_DIM) < (A_HEAD_DIM // 2)
    q_scale = A_HEAD_DIM ** -0.5
    step = 2 * LANES
    for c0 in range(0, d_qk, step):
        qc = jnp.dot(h, w_ref[:, c0:c0 + step], preferred_element_type=F32)
        kc = jnp.dot(h, w_ref[:, d_qk + c0:d_qk + c0 + step], preferred_element_type=F32)
        for l0 in range(0, step, LANES):
            qr = _rotate(qc[:, l0:l0 + LANES], cos, sin, use_up, A_HEAD_DIM // 2) * q_scale
            kr = _rotate(kc[:, l0:l0 + LANES], cos, sin, use_up, A_HEAD_DIM // 2)
            q_ref[:, c0 + l0:c0 + l0 + LANES] = qr.astype(BF16)
            k_ref[:, c0 + l0:c0 + l0 + LANES] = kr.astype(BF16)
    for c0 in range(0, d_v, step):
        vc = jnp.dot(h, w_ref[:, 2 * d_qk + c0:2 * d_qk + c0 + step], preferred_element_type=F32)
        v_ref[:, c0:c0 + step] = vc.astype(BF16)


def _attn_proj(x, nw, w, cos, sin, *, seq, tm):
    T, D = x.shape
    d_qk = D
    d_v = w.shape[1] - 2 * d_qk
    nseq = seq // tm
    row = lambda i: (i, 0)
    return pl.pallas_call(
        functools.partial(_attn_proj_kernel, d_qk=d_qk, d_v=d_v),
        grid=(T // tm,),
        in_specs=[
            pl.BlockSpec((tm, D), row),
            _resident((1, D)),
            _resident(w.shape),
            pl.BlockSpec((tm, LANES), lambda i: (i % nseq, 0)),
            pl.BlockSpec((tm, LANES), lambda i: (i % nseq, 0)),
        ],
        out_specs=[pl.BlockSpec((tm, d_qk), row), pl.BlockSpec((tm, d_qk), row),
                   pl.BlockSpec((tm, d_v), row)],
        out_shape=[jax.ShapeDtypeStruct((T, d_qk), BF16), jax.ShapeDtypeStruct((T, d_qk), BF16),
                   jax.ShapeDtypeStruct((T, d_v), BF16)],
        compiler_params=_params("parallel"),
        name="attn_proj",
    )(x, nw, w, cos, sin)


def _attn_core_kernel(q_ref, k_ref, v_ref, lq1_ref, lk1_ref, lq2_ref, lk2_ref, subln_ref, o_ref,
                      *, lam_init, tk, seq):
    lam = (jnp.exp(jnp.sum(lq1_ref[...] * lk1_ref[...], keepdims=True))
           - jnp.exp(jnp.sum(lq2_ref[...] * lk2_ref[...], keepdims=True)) + lam_init)
    q = q_ref[...]
    lane = lax.broadcasted_iota(jnp.int32, q.shape, 1)
    zero = jnp.zeros_like(q)
    qs = (jnp.where(lane < A_HEAD_DIM, q, zero), jnp.where(lane >= A_HEAD_DIM, q, zero))
    tq = q.shape[0]
    dv = v_ref.shape[1]

    def body(j, carry):
        kc = k_ref[pl.ds(pl.multiple_of(j * tk, tk), tk), :]
        vc = v_ref[pl.ds(pl.multiple_of(j * tk, tk), tk), :]
        new = []
        for c in range(2):
            m, l, acc = carry[c]
            s = lax.dot_general(qs[c], kc, (((1,), (1,)), ((), ())), preferred_element_type=F32)
            m_new = jnp.maximum(m, jnp.max(s, axis=-1, keepdims=True))
            alpha = jnp.exp(m - m_new)
            p = jnp.exp(s - m_new)
            l = alpha * l + jnp.sum(p, axis=-1, keepdims=True)
            acc = alpha * acc + jnp.dot(p.astype(BF16), vc, preferred_element_type=F32)
            new.append((m_new, l, acc))
        return tuple(new)

    init = tuple((jnp.full((tq, 1), -jnp.inf, F32), jnp.zeros((tq, 1), F32), jnp.zeros((tq, dv), F32))
                 for _ in range(2))
    (_, l1, a1), (_, l2, a2) = lax.fori_loop(0, seq // tk, body, init)
    o = a1 / l1 - lam * (a2 / l2)
    o = _rms(o, SUBLN_EPS) * subln_ref[...] * (1.0 - lam_init)
    o_ref[...] = o.astype(BF16)


def _attn_core(q, k, v, lq1, lk1, lq2, lk2, subln, *, batch, seq, lam_init, tq, tk):
    T, D = q.shape
    hd = 2 * A_HEAD_DIM
    heads = D // hd
    nq = seq // tq
    small = lambda shape: pl.BlockSpec(shape, lambda b, h, i: (0, 0))
    return pl.pallas_call(
        functools.partial(_attn_core_kernel, lam_init=lam_init, tk=tk, seq=seq),
        grid=(batch, heads, nq),
        in_specs=[
            pl.BlockSpec((tq, hd), lambda b, h, i: (b * nq + i, h)),
            pl.BlockSpec((seq, hd), lambda b, h, i: (b, h)),
            pl.BlockSpec((seq, hd), lambda b, h, i: (b, h)),
            small((1, A_HEAD_DIM)), small((1, A_HEAD_DIM)), small((1, A_HEAD_DIM)),
            small((1, A_HEAD_DIM)), small((1, hd)),
        ],
        out_specs=pl.BlockSpec((tq, hd), lambda b, h, i: (b * nq + i, h)),
        out_shape=jax.ShapeDtypeStruct((T, D), BF16),
        compiler_params=_params("parallel", "parallel", "arbitrary"),
        name="attn_core",
    )(q, k, v, lq1, lk1, lq2, lk2, subln)


def _ret_proj_kernel(x_ref, nw_ref, w_ref, cos_ref, sin_ref, q_ref, k_ref, v_ref, g_ref,
                     *, d_qk, d_v, d_head):
    h = (_rms(x_ref[...], NORM_EPS) * nw_ref[...]).astype(BF16)
    lane = lax.broadcasted_iota(jnp.int32, (x_ref.shape[0], LANES), 1)
    use_up = (lane % 2) == 0
    k_scale = d_head ** -0.5
    step = 2 * LANES
    for c0 in range(0, d_qk, step):
        qc = jnp.dot(h, w_ref[:, c0:c0 + step], preferred_element_type=F32)
        kc = jnp.dot(h, w_ref[:, d_qk + c0:d_qk + c0 + step], preferred_element_type=F32)
        for l0 in range(0, step, LANES):
            t0 = (c0 + l0) % d_head
            cos = cos_ref[:, t0:t0 + LANES]
            sin = sin_ref[:, t0:t0 + LANES]
            qr = _rotate(qc[:, l0:l0 + LANES], cos, sin, use_up, 1)
            kr = _rotate(kc[:, l0:l0 + LANES], cos, sin, use_up, 1) * k_scale
            q_ref[:, c0 + l0:c0 + l0 + LANES] = qr.astype(BF16)
            k_ref[:, c0 + l0:c0 + l0 + LANES] = kr.astype(BF16)
    for c0 in range(0, d_v, step):
        vc = jnp.dot(h, w_ref[:, 2 * d_qk + c0:2 * d_qk + c0 + step], preferred_element_type=F32)
        v_ref[:, c0:c0 + step] = vc.astype(BF16)
        g_ref[:, c0:c0 + step] = jnp.dot(h, w_ref[:, 2 * d_qk + d_v + c0:2 * d_qk + d_v + c0 + step],
                                         preferred_element_type=F32)


def _ret_proj(x, nw, w, cos, sin, *, seq, tm):
    T, D = x.shape
    d_qk = D
    d_v = (w.shape[1] - 2 * d_qk) // 2
    d_head = d_qk // R_HEADS
    nseq = seq // tm
    row = lambda i: (i, 0)
    return pl.pallas_call(
        functools.partial(_ret_proj_kernel, d_qk=d_qk, d_v=d_v, d_head=d_head),
        grid=(T // tm,),
        in_specs=[
            pl.BlockSpec((tm, D), row),
            _resident((1, D)),
            _resident(w.shape),
            pl.BlockSpec((tm, d_head), lambda i: (i % nseq, 0)),
            pl.BlockSpec((tm, d_head), lambda i: (i % nseq, 0)),
        ],
        out_specs=[pl.BlockSpec((tm, d_qk), row), pl.BlockSpec((tm, d_qk), row),
                   pl.BlockSpec((tm, d_v), row), pl.BlockSpec((tm, d_v), row)],
        out_shape=[jax.ShapeDtypeStruct((T, d_qk), BF16), jax.ShapeDtypeStruct((T, d_qk), BF16),
                   jax.ShapeDtypeStruct((T, d_v), BF16), jax.ShapeDtypeStruct((T, d_v), F32)],
        compiler_params=_params("parallel"),
        name="ret_proj",
    )(x, nw, w, cos, sin)


def _ret_core_kernel(dec_ref, q_ref, k_ref, v_ref, g_ref, o_ref, rf_ref, rb_ref, cb_ref, dm_ref,
                     *, chunk, rows, nblk):
    t = pl.program_id(2)
    nch = rows // chunk
    lg = -jnp.exp(dec_ref[...])
    lgf = lg[0:1, 0:1]
    lgb = lg[1:2, 0:1]
    pos = lax.broadcasted_iota(jnp.int32, (chunk, 1), 0).astype(F32)
    contract_rows = (((0,), (0,)), ((), ()))
    contract_last = (((1,), (1,)), ((), ()))

    @pl.when(t == 0)
    def _():
        rf_ref[...] = jnp.zeros_like(rf_ref)
        rb_ref[...] = jnp.zeros_like(rb_ref)
        diff = (lax.broadcasted_iota(jnp.int32, (chunk, chunk), 0)
                - lax.broadcasted_iota(jnp.int32, (chunk, chunk), 1)).astype(F32)
        dm_ref[...] = jnp.exp(jnp.where(diff >= 0, diff * lgf, -diff * lgb))

    @pl.when(t < nblk)
    def _():
        blk = nblk - 1 - t
        xi = jnp.exp((chunk - pos) * lgb)
        zeta = jnp.exp(pos * lgb)
        cdecay = jnp.exp(chunk * lgb)
        for c in reversed(range(nch)):
            r0 = c * chunk
            qc = q_ref[r0:r0 + chunk, :]
            kc = k_ref[r0:r0 + chunk, :]
            vc = v_ref[r0:r0 + chunk, :]
            state = rb_ref[...]
            cross = jnp.dot(qc, state.astype(BF16), preferred_element_type=F32) * xi
            cb_ref[pl.ds(pl.multiple_of(blk * rows + r0, chunk), chunk), :] = cross
            kz = (kc.astype(F32) * zeta).astype(BF16)
            rb_ref[...] = state * cdecay + lax.dot_general(kz, vc, contract_rows,
                                                           preferred_element_type=F32)

    @pl.when(t >= nblk)
    def _():
        blk = t - nblk
        xi = jnp.exp((pos + 1.0) * lgf)
        zeta = jnp.exp((chunk - 1.0 - pos) * lgf)
        cdecay = jnp.exp(chunk * lgf)
        for c in range(nch):
            r0 = c * chunk
            qc = q_ref[r0:r0 + chunk, :]
            kc = k_ref[r0:r0 + chunk, :]
            vc = v_ref[r0:r0 + chunk, :]
            s = lax.dot_general(qc, kc, contract_last, preferred_element_type=F32)
            inner = jnp.dot((s * dm_ref[...]).astype(BF16), vc, preferred_element_type=F32)
            state = rf_ref[...]
            cross = jnp.dot(qc, state.astype(BF16), preferred_element_type=F32) * xi
            o = inner + cross + cb_ref[pl.ds(pl.multiple_of(blk * rows + r0, chunk), chunk), :]
            o = _rms(o, NORM_EPS)
            gc = g_ref[r0:r0 + chunk, :]
            o_ref[r0:r0 + chunk, :] = (gc * _sigmoid(gc) * o).astype(BF16)
            kz = (kc.astype(F32) * zeta).astype(BF16)
            rf_ref[...] = state * cdecay + lax.dot_general(kz, vc, contract_rows,
                                                           preferred_element_type=F32)


def _ret_core(dec, q, k, v, g, *, batch, seq, chunk, rows):
    T, d_qk = q.shape
    d_v = v.shape[1]
    dk = d_qk // R_HEADS
    dv = d_v // R_HEADS
    nblk = seq // rows

    def seq_block(b, h, t):
        blk = jnp.where(t < nblk, nblk - 1 - t, t - nblk)
        return (b * nblk + blk, h)

    def fwd_block(b, h, t):
        return (b * nblk + jnp.maximum(t - nblk, 0), h)

    return pl.pallas_call(
        functools.partial(_ret_core_kernel, chunk=chunk, rows=rows, nblk=nblk),
        grid=(batch, R_HEADS, 2 * nblk),
        in_specs=[
            pl.BlockSpec((None, 2, chunk), lambda b, h, t: (h, 0, 0)),
            pl.BlockSpec((rows, dk), seq_block),
            pl.BlockSpec((rows, dk), seq_block),
            pl.BlockSpec((rows, dv), seq_block),
            pl.BlockSpec((rows, dv), fwd_block),
        ],
        out_specs=pl.BlockSpec((rows, dv), fwd_block),
        out_shape=jax.ShapeDtypeStruct((T, d_v), BF16),
        scratch_shapes=[
            pltpu.VMEM((dk, dv), F32),
            pltpu.VMEM((dk, dv), F32),
            pltpu.VMEM((seq, dv), F32),
            pltpu.VMEM((chunk, chunk), F32),
        ],
        compiler_params=_params("parallel", "parallel", "arbitrary"),
        name="ret_core",
    )(dec, q, k, v, g)


def _tail_kernel(x_ref, o_ref, p_ref, wo_ref, n_mix_ref, n_pre_ref, n_post_ref, w1_ref, w2_ref,
                 wg_ref, wp_ref, y_ref):
    m = jnp.dot(o_ref[...], wo_ref[...], preferred_element_type=F32)
    x = x_ref[...] + _rms(m, NORM_EPS) * n_mix_ref[...]
    h = (_rms(x, NORM_EPS) * n_pre_ref[...]).astype(BF16)
    u = jnp.maximum(jnp.dot(h, w1_ref[...], preferred_element_type=F32), 0.0)
    m = jnp.dot((u * u).astype(BF16), w2_ref[...], preferred_element_type=F32)
    x = x + _rms(m, NORM_EPS) * n_post_ref[...]
    gate = _sigmoid(jnp.dot(_rms(x, NORM_EPS).astype(BF16), wg_ref[...], preferred_element_type=F32))
    emb = jnp.dot(p_ref[...].astype(BF16), wp_ref[...], preferred_element_type=F32)
    y_ref[...] = x + emb * gate


def _tail(x, o, p, wo, n_mix, n_pre, n_post, w1, w2, wg, wp, *, tm):
    T, D = x.shape
    row = lambda i: (i, 0)
    return pl.pallas_call(
        _tail_kernel,
        grid=(T // tm,),
        in_specs=[
            pl.BlockSpec((tm, D), row),
            pl.BlockSpec((tm, o.shape[1]), row),
            pl.BlockSpec((tm, p.shape[1]), row),
            _resident(wo.shape), _resident((1, D)), _resident((1, D)), _resident((1, D)),
            _resident(w1.shape), _resident(w2.shape), _resident(wg.shape), _resident(wp.shape),
        ],
        out_specs=pl.BlockSpec((tm, D), row),
        out_shape=jax.ShapeDtypeStruct((T, D), F32),
        compiler_params=_params("parallel"),
        name="layer_tail",
    )(x, o, p, wo, n_mix, n_pre, n_post, w1, w2, wg, wp)


def _rope_tables(seq):
    dim = A_HEAD_DIM
    inv_freq = 1.0 / (ROPE_THETA ** (jnp.arange(0, dim, 2, dtype=F32) / dim))
    ang = jnp.arange(seq, dtype=F32)[:, None] * inv_freq[None, :]
    ang = jnp.concatenate([ang, ang], axis=-1)
    cos, sin = jnp.cos(ang), jnp.sin(ang)
    sign = jnp.where(jnp.arange(dim) < dim // 2, -1.0, 1.0).astype(F32)
    reps = LANES // dim
    return jnp.tile(cos, (1, reps)), jnp.tile(sin * sign, (1, reps))


def _xpos_tables(seq, dim):
    angle = 1.0 / (XPOS_BASE ** jnp.linspace(0.0, 1.0, dim // 2, dtype=F32))
    angle = jnp.repeat(angle, 2)
    ang = jnp.arange(seq, dtype=F32)[:, None] * angle[None, :]
    sign = jnp.where(jnp.arange(dim) % 2 == 0, -1.0, 1.0).astype(F32)
    return jnp.cos(ang), jnp.sin(ang) * sign


def _pick(n, candidates):
    for c in candidates:
        if n % c == 0:
            return c
    raise ValueError(f"no tile for extent {n}")


def _trunk(x, p, norm_pre_mix, norm_post_mix, norm_pre_mlp, norm_post_mlp, attn_w_qkv, attn_w_o,
           attn_lambda_q1, attn_lambda_k1, attn_lambda_q2, attn_lambda_k2, attn_subln, ret_w_in,
           ret_w_out, ret_decay, mlp_w_in, mlp_w_out, ple_w_proj, ple_w_gate):
    batch, seq, d_model = x.shape
    depth = p.shape[0]
    T = batch * seq
    x = x.reshape(T, d_model)
    p = p.reshape(depth, T, p.shape[-1])
    tm = _pick(seq, (512, 256, 128, 64, 32, 16, 8))
    tq = _pick(seq, (256, 128, 64, 32, 16, 8))
    tk = _pick(seq, (512, 256, 128))
    chunk = _pick(seq, (256, 128))
    rows = _pick(seq, (512, 256, 128))
    rope = _rope_tables(seq)
    xpos = _xpos_tables(seq, d_model // R_HEADS)
    vec = lambda a: a.reshape(1, -1)
    for i in range(depth):
        j = i // N_MIXERS
        if i % N_MIXERS == 0:
            q, k, v = _attn_proj(x, vec(norm_pre_mix[i]), attn_w_qkv[j].astype(BF16), *rope,
                                 seq=seq, tm=tm)
            lam_init = 0.8 - 0.6 * math.exp(-0.3 * i)
            o = _attn_core(q, k, v, vec(attn_lambda_q1[j]), vec(attn_lambda_k1[j]),
                           vec(attn_lambda_q2[j]), vec(attn_lambda_k2[j]), vec(attn_subln[j]),
                           batch=batch, seq=seq, lam_init=lam_init, tq=tq, tk=tk)
            wo = attn_w_o[j]
        else:
            q, k, v, g = _ret_proj(x, vec(norm_pre_mix[i]), ret_w_in[j].astype(BF16), *xpos,
                                   seq=seq, tm=tm)
            dec = jnp.broadcast_to(ret_decay[j].T[:, :, None], (R_HEADS, 2, chunk))
            o = _ret_core(dec, q, k, v, g, batch=batch, seq=seq, chunk=chunk, rows=rows)
            wo = ret_w_out[j]
        x = _tail(x, o, p[i], wo.astype(BF16), vec(norm_post_mix[i]), vec(norm_pre_mlp[i]),
                  vec(norm_post_mlp[i]), mlp_w_in[i].astype(BF16), mlp_w_out[i].astype(BF16),
                  ple_w_gate[i].astype(BF16), ple_w_proj[i].astype(BF16), tm=tm)
    return x.reshape(batch, seq, d_model)


def kernel(x_prompt, x_sample, p_prompt, p_sample, norm_pre_mix, norm_post_mix, norm_pre_mlp, norm_post_mlp, attn_w_qkv, attn_w_o, attn_lambda_q1, attn_lambda_k1, attn_lambda_q2, attn_lambda_k2, attn_subln, ret_w_in, ret_w_out, ret_decay, mlp_w_in, mlp_w_out, ple_w_proj, ple_w_gate):
    weights = (norm_pre_mix, norm_post_mix, norm_pre_mlp, norm_post_mlp, attn_w_qkv, attn_w_o,
               attn_lambda_q1, attn_lambda_k1, attn_lambda_q2, attn_lambda_k2, attn_subln, ret_w_in,
               ret_w_out, ret_decay, mlp_w_in, mlp_w_out, ple_w_proj, ple_w_gate)
    return (_trunk(x_prompt, p_prompt, *weights), _trunk(x_sample, p_sample, *weights))
```

```python
import functools
import math

import jax
import jax.numpy as jnp
from jax import lax
from jax.experimental import pallas as pl
from jax.experimental.pallas import tpu as pltpu

F32 = jnp.float32
BF16 = jnp.bfloat16

LANES = 128
V7X_VMEM_BYTES = 64 * 2 ** 20
VMEM_LIMIT = V7X_VMEM_BYTES * 7 // 8

N_MIXERS = 2
A_HEAD_DIM = 64
R_HEADS = 4
ROPE_THETA = 10000.0
XPOS_BASE = 10000.0
LOG2E = math.log2(math.e)
NORM_EPS = 1e-6
SUBLN_EPS = 1e-5


def _params(*semantics):
    return pltpu.CompilerParams(dimension_semantics=semantics, vmem_limit_bytes=VMEM_LIMIT)


def _resident(shape):
    nd = len(shape)
    return pl.BlockSpec(shape, lambda *_: (0,) * nd, pipeline_mode=pl.Buffered(1))


def _rms(x, eps):
    return x * lax.rsqrt(jnp.mean(x * x, axis=-1, keepdims=True) + eps)


def _sigmoid(z):
    return 1.0 / (1.0 + jnp.exp(-z))


def _rotate(t, cos, sin_signed, use_up, shift):
    up = pltpu.roll(t, LANES - shift, 1)
    down = pltpu.roll(t, shift, 1)
    return t * cos + jnp.where(use_up, up, down) * sin_signed


def _attn_proj_kernel(x_ref, nw_ref, wqk_ref, wvt_ref, cos_ref, sin_ref, q_ref, k_ref, vt_ref, *, d_qk):
    h = (_rms(x_ref[...], NORM_EPS) * nw_ref[...]).astype(BF16)
    cos = cos_ref[...]
    sin = sin_ref[...]
    lane = lax.broadcasted_iota(jnp.int32, cos.shape, 1)
    use_up = (lane % A_HEAD_DIM) < (A_HEAD_DIM // 2)
    q_scale = A_HEAD_DIM ** -0.5 * LOG2E
    step = 2 * LANES
    for c0 in range(0, d_qk, step):
        qc = jnp.dot(h, wqk_ref[:, c0:c0 + step], preferred_element_type=F32)
        kc = jnp.dot(h, wqk_ref[:, d_qk + c0:d_qk + c0 + step], preferred_element_type=F32)
        for l0 in range(0, step, LANES):
            qr = _rotate(qc[:, l0:l0 + LANES], cos, sin, use_up, A_HEAD_DIM // 2) * q_scale
            kr = _rotate(kc[:, l0:l0 + LANES], cos, sin, use_up, A_HEAD_DIM // 2)
            q_ref[:, c0 + l0:c0 + l0 + LANES] = qr.astype(BF16)
            k_ref[:, c0 + l0:c0 + l0 + LANES] = kr.astype(BF16)
    for c0 in range(0, wvt_ref.shape[0], step):
        vt = lax.dot_general(wvt_ref[c0:c0 + step, :], h, (((1,), (1,)), ((), ())),
                             preferred_element_type=F32)
        vt_ref[c0:c0 + step, :] = vt.astype(BF16)


def _attn_proj(x, nw, wqk, wvt, cos, sin, *, seq, tm):
    T, D = x.shape
    d_qk = D
    d_v = wvt.shape[0]
    nseq = seq // tm
    row = lambda i: (i, 0)
    return pl.pallas_call(
        functools.partial(_attn_proj_kernel, d_qk=d_qk),
        grid=(T // tm,),
        in_specs=[
            pl.BlockSpec((tm, D), row),
            _resident((1, D)),
            _resident(wqk.shape),
            _resident(wvt.shape),
            pl.BlockSpec((tm, LANES), lambda i: (i % nseq, 0)),
            pl.BlockSpec((tm, LANES), lambda i: (i % nseq, 0)),
        ],
        out_specs=[pl.BlockSpec((tm, d_qk), row), pl.BlockSpec((tm, d_qk), row),
                   pl.BlockSpec((d_v, tm), lambda i: (0, i))],
        out_shape=[jax.ShapeDtypeStruct((T, d_qk), BF16), jax.ShapeDtypeStruct((T, d_qk), BF16),
                   jax.ShapeDtypeStruct((d_v, T), BF16)],
        compiler_params=_params("parallel"),
        name="attn_proj",
    )(x, nw, wqk, wvt, cos, sin)


def _attn_core_kernel(q_ref, k_ref, vt_ref, lq1_ref, lk1_ref, lq2_ref, lk2_ref, subln_ref, o_ref,
                      s_ref, p_ref, acc_ref, *, lam_init, tk, seq):
    lam = (jnp.exp(jnp.sum(lq1_ref[...] * lk1_ref[...], keepdims=True))
           - jnp.exp(jnp.sum(lq2_ref[...] * lk2_ref[...], keepdims=True)) + lam_init)
    qt = q_ref[...].astype(F32).T
    row = lax.broadcasted_iota(jnp.int32, qt.shape, 0)
    qts = (jnp.where(row < A_HEAD_DIM, qt, 0.0).astype(BF16),
           jnp.where(row >= A_HEAD_DIM, qt, 0.0).astype(BF16))
    tq = qt.shape[1]
    n_tiles = seq // tk

    def scores(j, slot):
        kc = k_ref[pl.ds(pl.multiple_of(j * tk, tk), tk), :]
        tile_max = []
        for c in range(2):
            s = jnp.dot(kc, qts[c], preferred_element_type=F32)
            s_ref[slot, c] = s
            tile_max.append(jnp.max(s, axis=0, keepdims=True))
        return tuple(tile_max)

    def probs(slot, tile_max, ms, ls):
        ms_new, ls_new, alphas = [], [], []
        for c in range(2):
            m_new = jnp.maximum(ms[c], tile_max[c])
            alpha = jnp.exp2(ms[c] - m_new)
            p = jnp.exp2(s_ref[slot, c] - m_new)
            p_ref[slot, c] = p.astype(BF16)
            ms_new.append(m_new)
            ls_new.append(alpha * ls[c] + jnp.sum(p, axis=0, keepdims=True))
            alphas.append(alpha)
        return tuple(ms_new), tuple(ls_new), tuple(alphas)

    def accumulate(j, slot, alphas):
        vtc = vt_ref[:, pl.ds(pl.multiple_of(j * tk, tk), tk)]
        for c in range(2):
            acc_ref[c] = alphas[c] * acc_ref[c] + jnp.dot(vtc, p_ref[slot, c],
                                                          preferred_element_type=F32)

    acc_ref[...] = jnp.zeros_like(acc_ref)
    ms = (jnp.full((1, tq), -jnp.inf, F32),) * 2
    ls = (jnp.zeros((1, tq), F32),) * 2
    tile_max = scores(0, 0)
    ms, ls, alphas = probs(0, tile_max, ms, ls)
    tile_max = scores(1, 1)

    def pair(i, carry):
        ms, ls, alphas, tile_max = carry
        j = 2 * i + 1
        accumulate(j - 1, 0, alphas)
        ms, ls, alphas = probs(1, tile_max, ms, ls)
        tile_max = scores(j + 1, 0)
        accumulate(j, 1, alphas)
        ms, ls, alphas = probs(0, tile_max, ms, ls)
        tile_max = scores(j + 2, 1)
        return ms, ls, alphas, tile_max

    ms, ls, alphas, tile_max = lax.fori_loop(0, n_tiles // 2 - 1, pair, (ms, ls, alphas, tile_max))
    accumulate(n_tiles - 2, 0, alphas)
    ms, ls, alphas = probs(1, tile_max, ms, ls)
    accumulate(n_tiles - 1, 1, alphas)

    o = (acc_ref[0] / ls[0] - lam * (acc_ref[1] / ls[1])).T
    o = _rms(o, SUBLN_EPS) * subln_ref[...] * (1.0 - lam_init)
    o_ref[...] = o.astype(BF16)


def _attn_core(q, k, vt, lq1, lk1, lq2, lk2, subln, *, batch, seq, lam_init, tq, tk):
    T, D = q.shape
    hd = 2 * A_HEAD_DIM
    heads = D // hd
    nq = seq // tq
    small = lambda shape: pl.BlockSpec(shape, lambda b, h, i: (0, 0))
    return pl.pallas_call(
        functools.partial(_attn_core_kernel, lam_init=lam_init, tk=tk, seq=seq),
        grid=(batch, heads, nq),
        in_specs=[
            pl.BlockSpec((tq, hd), lambda b, h, i: (b * nq + i, h)),
            pl.BlockSpec((seq, hd), lambda b, h, i: (b, h)),
            pl.BlockSpec((hd, seq), lambda b, h, i: (h, b)),
            small((1, A_HEAD_DIM)), small((1, A_HEAD_DIM)), small((1, A_HEAD_DIM)),
            small((1, A_HEAD_DIM)), small((1, hd)),
        ],
        out_specs=pl.BlockSpec((tq, hd), lambda b, h, i: (b * nq + i, h)),
        out_shape=jax.ShapeDtypeStruct((T, D), BF16),
        scratch_shapes=[
            pltpu.VMEM((2, 2, tk, tq), F32),
            pltpu.VMEM((2, 2, tk, tq), BF16),
            pltpu.VMEM((2, hd, tq), F32),
        ],
        compiler_params=_params("parallel", "parallel", "arbitrary"),
        name="attn_core",
    )(q, k, vt, lq1, lk1, lq2, lk2, subln)


def _ret_proj_kernel(x_ref, nw_ref, w_ref, cos_ref, sin_ref, q_ref, k_ref, v_ref, g_ref,
                     *, d_qk, d_v, d_head):
    h = (_rms(x_ref[...], NORM_EPS) * nw_ref[...]).astype(BF16)
    lane = lax.broadcasted_iota(jnp.int32, (x_ref.shape[0], LANES), 1)
    use_up = (lane % 2) == 0
    k_scale = d_head ** -0.5
    step = 2 * LANES
    for c0 in range(0, d_qk, step):
        qc = jnp.dot(h, w_ref[:, c0:c0 + step], preferred_element_type=F32)
        kc = jnp.dot(h, w_ref[:, d_qk + c0:d_qk + c0 + step], preferred_element_type=F32)
        for l0 in range(0, step, LANES):
            t0 = (c0 + l0) % d_head
            cos = cos_ref[:, t0:t0 + LANES]
            sin = sin_ref[:, t0:t0 + LANES]
            qr = _rotate(qc[:, l0:l0 + LANES], cos, sin, use_up, 1)
            kr = _rotate(kc[:, l0:l0 + LANES], cos, sin, use_up, 1) * k_scale
            q_ref[:, c0 + l0:c0 + l0 + LANES] = qr.astype(BF16)
            k_ref[:, c0 + l0:c0 + l0 + LANES] = kr.astype(BF16)
    for c0 in range(0, d_v, step):
        vc = jnp.dot(h, w_ref[:, 2 * d_qk + c0:2 * d_qk + c0 + step], preferred_element_type=F32)
        v_ref[:, c0:c0 + step] = vc.astype(BF16)
        g_ref[:, c0:c0 + step] = jnp.dot(h, w_ref[:, 2 * d_qk + d_v + c0:2 * d_qk + d_v + c0 + step],
                                         preferred_element_type=F32)


def _ret_proj(x, nw, w, cos, sin, *, seq, tm):
    T, D = x.shape
    d_qk = D
    d_v = (w.shape[1] - 2 * d_qk) // 2
    d_head = d_qk // R_HEADS
    nseq = seq // tm
    row = lambda i: (i, 0)
    return pl.pallas_call(
        functools.partial(_ret_proj_kernel, d_qk=d_qk, d_v=d_v, d_head=d_head),
        grid=(T // tm,),
        in_specs=[
            pl.BlockSpec((tm, D), row),
            _resident((1, D)),
            _resident(w.shape),
            pl.BlockSpec((tm, d_head), lambda i: (i % nseq, 0)),
            pl.BlockSpec((tm, d_head), lambda i: (i % nseq, 0)),
        ],
        out_specs=[pl.BlockSpec((tm, d_qk), row), pl.BlockSpec((tm, d_qk), row),
                   pl.BlockSpec((tm, d_v), row), pl.BlockSpec((tm, d_v), row)],
        out_shape=[jax.ShapeDtypeStruct((T, d_qk), BF16), jax.ShapeDtypeStruct((T, d_qk), BF16),
                   jax.ShapeDtypeStruct((T, d_v), BF16), jax.ShapeDtypeStruct((T, d_v), F32)],
        compiler_params=_params("parallel"),
        name="ret_proj",
    )(x, nw, w, cos, sin)


def _ret_core_kernel(dec_ref, q_ref, k_ref, v_ref, g_ref, o_ref, rf_ref, rb_ref, cb_ref, dm_ref,
                     *, chunk, rows, nblk):
    t = pl.program_id(2)
    nch = rows // chunk
    lg = -jnp.exp(dec_ref[...])
    lgf = lg[0:1, 0:1]
    lgb = lg[1:2, 0:1]
    pos = lax.broadcasted_iota(jnp.int32, (chunk, 1), 0).astype(F32)
    contract_rows = (((0,), (0,)), ((), ()))
    contract_last = (((1,), (1,)), ((), ()))

    @pl.when(t == 0)
    def _():
        rf_ref[...] = jnp.zeros_like(rf_ref)
        rb_ref[...] = jnp.zeros_like(rb_ref)
        diff = (lax.broadcasted_iota(jnp.int32, (chunk, chunk), 0)
                - lax.broadcasted_iota(jnp.int32, (chunk, chunk), 1)).astype(F32)
        dm_ref[...] = jnp.exp(jnp.where(diff >= 0, diff * lgf, -diff * lgb))

    @pl.when(t < nblk)
    def _():
        blk = nblk - 1 - t
        xi = jnp.exp((chunk - pos) * lgb)
        zeta = jnp.exp(pos * lgb)
        cdecay = jnp.exp(chunk * lgb)
        for c in reversed(range(nch)):
            r0 = c * chunk
            qc = q_ref[r0:r0 + chunk, :]
            kc = k_ref[r0:r0 + chunk, :]
            vc = v_ref[r0:r0 + chunk, :]
            state = rb_ref[...]
            cross = jnp.dot(qc, state.astype(BF16), preferred_element_type=F32) * xi
            cb_ref[pl.ds(pl.multiple_of(blk * rows + r0, chunk), chunk), :] = cross
            kz = (kc.astype(F32) * zeta).astype(BF16)
            rb_ref[...] = state * cdecay + lax.dot_general(kz, vc, contract_rows,
                                                           preferred_element_type=F32)

    @pl.when(t >= nblk)
    def _():
        blk = t - nblk
        xi = jnp.exp((pos + 1.0) * lgf)
        zeta = jnp.exp((chunk - 1.0 - pos) * lgf)
        cdecay = jnp.exp(chunk * lgf)
        for c in range(nch):
            r0 = c * chunk
            qc = q_ref[r0:r0 + chunk, :]
            kc = k_ref[r0:r0 + chunk, :]
            vc = v_ref[r0:r0 + chunk, :]
            s = lax.dot_general(qc, kc, contract_last, preferred_element_type=F32)
            inner = jnp.dot((s * dm_ref[...]).astype(BF16), vc, preferred_element_type=F32)
            state = rf_ref[...]
            cross = jnp.dot(qc, state.astype(BF16), preferred_element_type=F32) * xi
            o = inner + cross + cb_ref[pl.ds(pl.multiple_of(blk * rows + r0, chunk), chunk), :]
            o = _rms(o, NORM_EPS)
            gc = g_ref[r0:r0 + chunk, :]
            o_ref[r0:r0 + chunk, :] = (gc * _sigmoid(gc) * o).astype(BF16)
            kz = (kc.astype(F32) * zeta).astype(BF16)
            rf_ref[...] = state * cdecay + lax.dot_general(kz, vc, contract_rows,
                                                           preferred_element_type=F32)


def _ret_core(dec, q, k, v, g, *, batch, seq, chunk, rows):
    T, d_qk = q.shape
    d_v = v.shape[1]
    dk = d_qk // R_HEADS
    dv = d_v // R_HEADS
    nblk = seq // rows

    def seq_block(b, h, t):
        blk = jnp.where(t < nblk, nblk - 1 - t, t - nblk)
        return (b * nblk + blk, h)

    def fwd_block(b, h, t):
        return (b * nblk + jnp.maximum(t - nblk, 0), h)

    return pl.pallas_call(
        functools.partial(_ret_core_kernel, chunk=chunk, rows=rows, nblk=nblk),
        grid=(batch, R_HEADS, 2 * nblk),
        in_specs=[
            pl.BlockSpec((None, 2, chunk), lambda b, h, t: (h, 0, 0)),
            pl.BlockSpec((rows, dk), seq_block),
            pl.BlockSpec((rows, dk), seq_block),
            pl.BlockSpec((rows, dv), seq_block),
            pl.BlockSpec((rows, dv), fwd_block),
        ],
        out_specs=pl.BlockSpec((rows, dv), fwd_block),
        out_shape=jax.ShapeDtypeStruct((T, d_v), BF16),
        scratch_shapes=[
            pltpu.VMEM((dk, dv), F32),
            pltpu.VMEM((dk, dv), F32),
            pltpu.VMEM((seq, dv), F32),
            pltpu.VMEM((chunk, chunk), F32),
        ],
        compiler_params=_params("parallel", "parallel", "arbitrary"),
        name="ret_core",
    )(dec, q, k, v, g)


def _tail_kernel(x_ref, o_ref, p_ref, wo_ref, n_mix_ref, n_pre_ref, n_post_ref, w1_ref, w2_ref,
                 wg_ref, wp_ref, y_ref):
    m = jnp.dot(o_ref[...], wo_ref[...], preferred_element_type=F32)
    x = x_ref[...] + _rms(m, NORM_EPS) * n_mix_ref[...]
    h = (_rms(x, NORM_EPS) * n_pre_ref[...]).astype(BF16)
    u = jnp.maximum(jnp.dot(h, w1_ref[...], preferred_element_type=F32), 0.0)
    m = jnp.dot((u * u).astype(BF16), w2_ref[...], preferred_element_type=F32)
    x = x + _rms(m, NORM_EPS) * n_post_ref[...]
    gate = _sigmoid(jnp.dot(_rms(x, NORM_EPS).astype(BF16), wg_ref[...], preferred_element_type=F32))
    emb = jnp.dot(p_ref[...].astype(BF16), wp_ref[...], preferred_element_type=F32)
    y_ref[...] = x + emb * gate


def _tail(x, o, p, wo, n_mix, n_pre, n_post, w1, w2, wg, wp, *, tm):
    T, D = x.shape
    row = lambda i: (i, 0)
    return pl.pallas_call(
        _tail_kernel,
        grid=(T // tm,),
        in_specs=[
            pl.BlockSpec((tm, D), row),
            pl.BlockSpec((tm, o.shape[1]), row),
            pl.BlockSpec((tm, p.shape[1]), row),
            _resident(wo.shape), _resident((1, D)), _resident((1, D)), _resident((1, D)),
            _resident(w1.shape), _resident(w2.shape), _resident(wg.shape), _resident(wp.shape),
        ],
        out_specs=pl.BlockSpec((tm, D), row),
        out_shape=jax.ShapeDtypeStruct((T, D), F32),
        compiler_params=_params("parallel"),
        name="layer_tail",
    )(x, o, p, wo, n_mix, n_pre, n_post, w1, w2, wg, wp)


def _rope_tables(seq):
    dim = A_HEAD_DIM
    inv_freq = 1.0 / (ROPE_THETA ** (jnp.arange(0, dim, 2, dtype=F32) / dim))
    ang = jnp.arange(seq, dtype=F32)[:, None] * inv_freq[None, :]
    ang = jnp.concatenate([ang, ang], axis=-1)
    cos, sin = jnp.cos(ang), jnp.sin(ang)
    sign = jnp.where(jnp.arange(dim) < dim // 2, -1.0, 1.0).astype(F32)
    reps = LANES // dim
    return jnp.tile(cos, (1, reps)), jnp.tile(sin * sign, (1, reps))


def _xpos_tables(seq, dim):
    angle = 1.0 / (XPOS_BASE ** jnp.linspace(0.0, 1.0, dim // 2, dtype=F32))
    angle = jnp.repeat(angle, 2)
    ang = jnp.arange(seq, dtype=F32)[:, None] * angle[None, :]
    sign = jnp.where(jnp.arange(dim) % 2 == 0, -1.0, 1.0).astype(F32)
    return jnp.cos(ang), jnp.sin(ang) * sign


def _pick(n, candidates):
    for c in candidates:
        if n % c == 0:
            return c
    raise ValueError(f"no tile for extent {n}")


def _trunk(x, p, norm_pre_mix, norm_post_mix, norm_pre_mlp, norm_post_mlp, attn_w_qkv, attn_w_o,
           attn_lambda_q1, attn_lambda_k1, attn_lambda_q2, attn_lambda_k2, attn_subln, ret_w_in,
           ret_w_out, ret_decay, mlp_w_in, mlp_w_out, ple_w_proj, ple_w_gate):
    batch, seq, d_model = x.shape
    depth = p.shape[0]
    T = batch * seq
    x = x.reshape(T, d_model)
    p = p.reshape(depth, T, p.shape[-1])
    tm = _pick(seq, (512, 256, 128, 64, 32, 16, 8))
    tq = _pick(seq, (256, 128, 64, 32, 16, 8))
    tk = _pick(seq, (512, 256, 128))
    chunk = _pick(seq, (256, 128))
    rows = _pick(seq, (512, 256, 128))
    rope = _rope_tables(seq)
    xpos = _xpos_tables(seq, d_model // R_HEADS)
    vec = lambda a: a.reshape(1, -1)
    for i in range(depth):
        j = i // N_MIXERS
        if i % N_MIXERS == 0:
            wqkv = attn_w_qkv[j].astype(BF16)
            q, k, v = _attn_proj(x, vec(norm_pre_mix[i]), wqkv[:, :2 * d_model],
                                 wqkv[:, 2 * d_model:].T, *rope, seq=seq, tm=tm)
            lam_init = 0.8 - 0.6 * math.exp(-0.3 * i)
            o = _attn_core(q, k, v, vec(attn_lambda_q1[j]), vec(attn_lambda_k1[j]),
                           vec(attn_lambda_q2[j]), vec(attn_lambda_k2[j]), vec(attn_subln[j]),
                           batch=batch, seq=seq, lam_init=lam_init, tq=tq, tk=tk)
            wo = attn_w_o[j]
        else:
            q, k, v, g = _ret_proj(x, vec(norm_pre_mix[i]), ret_w_in[j].astype(BF16), *xpos,
                                   seq=seq, tm=tm)
            dec = jnp.broadcast_to(ret_decay[j].T[:, :, None], (R_HEADS, 2, chunk))
            o = _ret_core(dec, q, k, v, g, batch=batch, seq=seq, chunk=chunk, rows=rows)
            wo = ret_w_out[j]
        x = _tail(x, o, p[i], wo.astype(BF16), vec(norm_post_mix[i]), vec(norm_pre_mlp[i]),
                  vec(norm_post_mlp[i]), mlp_w_in[i].astype(BF16), mlp_w_out[i].astype(BF16),
                  ple_w_gate[i].astype(BF16), ple_w_proj[i].astype(BF16), tm=tm)
    return x.reshape(batch, seq, d_model)


def kernel(x_prompt, x_sample, p_prompt, p_sample, norm_pre_mix, norm_post_mix, norm_pre_mlp, norm_post_mlp, attn_w_qkv, attn_w_o, attn_lambda_q1, attn_lambda_k1, attn_lambda_q2, attn_lambda_k2, attn_subln, ret_w_in, ret_w_out, ret_decay, mlp_w_in, mlp_w_out, ple_w_proj, ple_w_gate):
    weights = (norm_pre_mix, norm_post_mix, norm_pre_mlp, norm_post_mlp, attn_w_qkv, attn_w_o,
               attn_lambda_q1, attn_lambda_k1, attn_lambda_q2, attn_lambda_k2, attn_subln, ret_w_in,
               ret_w_out, ret_decay, mlp_w_in, mlp_w_out, ple_w_proj, ple_w_gate)
    return (_trunk(x_prompt, p_prompt, *weights), _trunk(x_sample, p_sample, *weights))
```

```python
import functools
import math

import jax
import jax.numpy as jnp
from jax import lax
from jax.experimental import pallas as pl
from jax.experimental.pallas import tpu as pltpu

F32 = jnp.float32
BF16 = jnp.bfloat16

LANES = 128
V7X_VMEM_BYTES = 64 * 2 ** 20
VMEM_LIMIT = V7X_VMEM_BYTES * 7 // 8

N_MIXERS = 2
A_HEAD_DIM = 64
R_HEADS = 4
ROPE_THETA = 10000.0
XPOS_BASE = 10000.0
LOG2E = math.log2(math.e)
NORM_EPS = 1e-6
SUBLN_EPS = 1e-5


def _params(*semantics):
    return pltpu.CompilerParams(dimension_semantics=semantics, vmem_limit_bytes=VMEM_LIMIT)


def _resident(shape):
    nd = len(shape)
    return pl.BlockSpec(shape, lambda *_: (0,) * nd, pipeline_mode=pl.Buffered(1))


def _rms(x, eps):
    return x * lax.rsqrt(jnp.mean(x * x, axis=-1, keepdims=True) + eps)


def _sigmoid(z):
    return 1.0 / (1.0 + jnp.exp(-z))


def _rotate(t, cos, sin_signed, use_up, shift):
    up = pltpu.roll(t, LANES - shift, 1)
    down = pltpu.roll(t, shift, 1)
    return t * cos + jnp.where(use_up, up, down) * sin_signed


def _attn_proj_kernel(x_ref, nw_ref, wqk_ref, wvt_ref, cos_ref, sin_ref, q_ref, k_ref, vt_ref, *, d_qk):
    h = (_rms(x_ref[...], NORM_EPS) * nw_ref[...]).astype(BF16)
    cos = cos_ref[...]
    sin = sin_ref[...]
    lane = lax.broadcasted_iota(jnp.int32, cos.shape, 1)
    use_up = (lane % A_HEAD_DIM) < (A_HEAD_DIM // 2)
    q_scale = A_HEAD_DIM ** -0.5 * LOG2E
    step = 2 * LANES
    for c0 in range(0, d_qk, step):
        qc = jnp.dot(h, wqk_ref[:, c0:c0 + step], preferred_element_type=F32)
        kc = jnp.dot(h, wqk_ref[:, d_qk + c0:d_qk + c0 + step], preferred_element_type=F32)
        for l0 in range(0, step, LANES):
            qr = _rotate(qc[:, l0:l0 + LANES], cos, sin, use_up, A_HEAD_DIM // 2) * q_scale
            kr = _rotate(kc[:, l0:l0 + LANES], cos, sin, use_up, A_HEAD_DIM // 2)
            q_ref[:, c0 + l0:c0 + l0 + LANES] = qr.astype(BF16)
            k_ref[:, c0 + l0:c0 + l0 + LANES] = kr.astype(BF16)
    for c0 in range(0, wvt_ref.shape[0], step):
        vt = lax.dot_general(wvt_ref[c0:c0 + step, :], h, (((1,), (1,)), ((), ())),
                             preferred_element_type=F32)
        vt_ref[c0:c0 + step, :] = vt.astype(BF16)


def _attn_proj(x, nw, wqk, wvt, cos, sin, *, seq, tm):
    T, D = x.shape
    d_qk = D
    d_v = wvt.shape[0]
    nseq = seq // tm
    row = lambda i: (i, 0)
    return pl.pallas_call(
        functools.partial(_attn_proj_kernel, d_qk=d_qk),
        grid=(T // tm,),
        in_specs=[
            pl.BlockSpec((tm, D), row),
            _resident((1, D)),
            _resident(wqk.shape),
            _resident(wvt.shape),
            pl.BlockSpec((tm, LANES), lambda i: (i % nseq, 0)),
            pl.BlockSpec((tm, LANES), lambda i: (i % nseq, 0)),
        ],
        out_specs=[pl.BlockSpec((tm, d_qk), row), pl.BlockSpec((tm, d_qk), row),
                   pl.BlockSpec((d_v, tm), lambda i: (0, i))],
        out_shape=[jax.ShapeDtypeStruct((T, d_qk), BF16), jax.ShapeDtypeStruct((T, d_qk), BF16),
                   jax.ShapeDtypeStruct((d_v, T), BF16)],
        compiler_params=_params("parallel"),
        name="attn_proj",
    )(x, nw, wqk, wvt, cos, sin)


def _attn_core_kernel(q_ref, k_ref, vt_ref, lq1_ref, lk1_ref, lq2_ref, lk2_ref, subln_ref, o_ref,
                      s_ref, p_ref, acc_ref, *, lam_init, tk, seq):
    lam = (jnp.exp(jnp.sum(lq1_ref[...] * lk1_ref[...], keepdims=True))
           - jnp.exp(jnp.sum(lq2_ref[...] * lk2_ref[...], keepdims=True)) + lam_init)
    qt = q_ref[...].astype(F32).T
    row = lax.broadcasted_iota(jnp.int32, qt.shape, 0)
    qts = (jnp.where(row < A_HEAD_DIM, qt, 0.0).astype(BF16),
           jnp.where(row >= A_HEAD_DIM, qt, 0.0).astype(BF16))
    tq = qt.shape[1]
    n_tiles = seq // tk

    def scores(j, slot):
        kc = k_ref[j * tk:(j + 1) * tk, :]
        tile_max = []
        for c in range(2):
            s = jnp.dot(kc, qts[c], preferred_element_type=F32)
            s_ref[slot, c] = s
            tile_max.append(jnp.max(s, axis=0, keepdims=True))
        return tuple(tile_max)

    def probs(slot, tile_max, ms, ls):
        ms_new, ls_new, alphas = [], [], []
        for c in range(2):
            m_new = jnp.maximum(ms[c], tile_max[c])
            alpha = jnp.exp2(ms[c] - m_new)
            p = jnp.exp2(s_ref[slot, c] - m_new)
            p_ref[slot, c] = p.astype(BF16)
            ms_new.append(m_new)
            ls_new.append(alpha * ls[c] + jnp.sum(p, axis=0, keepdims=True))
            alphas.append(alpha)
        return tuple(ms_new), tuple(ls_new), tuple(alphas)

    def accumulate(j, slot, alphas):
        vtc = vt_ref[:, j * tk:(j + 1) * tk]
        for c in range(2):
            acc_ref[c] = alphas[c] * acc_ref[c] + jnp.dot(vtc, p_ref[slot, c],
                                                          preferred_element_type=F32)

    acc_ref[...] = jnp.zeros_like(acc_ref)
    ms = (jnp.full((1, tq), -jnp.inf, F32),) * 2
    ls = (jnp.zeros((1, tq), F32),) * 2
    tile_max = scores(0, 0)
    for j in range(n_tiles):
        if j + 1 < n_tiles:
            next_max = scores(j + 1, (j + 1) % 2)
        if j > 0:
            accumulate(j - 1, (j - 1) % 2, alphas)
        ms, ls, alphas = probs(j % 2, tile_max, ms, ls)
        tile_max = next_max
    accumulate(n_tiles - 1, (n_tiles - 1) % 2, alphas)

    o = (acc_ref[0] / ls[0] - lam * (acc_ref[1] / ls[1])).T
    o = _rms(o, SUBLN_EPS) * subln_ref[...] * (1.0 - lam_init)
    o_ref[...] = o.astype(BF16)


def _attn_core(q, k, vt, lq1, lk1, lq2, lk2, subln, *, batch, seq, lam_init, tq, tk):
    T, D = q.shape
    hd = 2 * A_HEAD_DIM
    heads = D // hd
    nq = seq // tq
    small = lambda shape: pl.BlockSpec(shape, lambda b, h, i: (0, 0))
    return pl.pallas_call(
        functools.partial(_attn_core_kernel, lam_init=lam_init, tk=tk, seq=seq),
        grid=(batch, heads, nq),
        in_specs=[
            pl.BlockSpec((tq, hd), lambda b, h, i: (b * nq + i, h)),
            pl.BlockSpec((seq, hd), lambda b, h, i: (b, h)),
            pl.BlockSpec((hd, seq), lambda b, h, i: (h, b)),
            small((1, A_HEAD_DIM)), small((1, A_HEAD_DIM)), small((1, A_HEAD_DIM)),
            small((1, A_HEAD_DIM)), small((1, hd)),
        ],
        out_specs=pl.BlockSpec((tq, hd), lambda b, h, i: (b * nq + i, h)),
        out_shape=jax.ShapeDtypeStruct((T, D), BF16),
        scratch_shapes=[
            pltpu.VMEM((2, 2, tk, tq), F32),
            pltpu.VMEM((2, 2, tk, tq), BF16),
            pltpu.VMEM((2, hd, tq), F32),
        ],
        compiler_params=_params("parallel", "parallel", "arbitrary"),
        name="attn_core",
    )(q, k, vt, lq1, lk1, lq2, lk2, subln)


def _ret_proj_kernel(x_ref, nw_ref, w_ref, cos_ref, sin_ref, q_ref, k_ref, v_ref, g_ref,
                     *, d_qk, d_v, d_head):
    h = (_rms(x_ref[...], NORM_EPS) * nw_ref[...]).astype(BF16)
    lane = lax.broadcasted_iota(jnp.int32, (x_ref.shape[0], LANES), 1)
    use_up = (lane % 2) == 0
    k_scale = d_head ** -0.5
    step = 2 * LANES
    for c0 in range(0, d_qk, step):
        qc = jnp.dot(h, w_ref[:, c0:c0 + step], preferred_element_type=F32)
        kc = jnp.dot(h, w_ref[:, d_qk + c0:d_qk + c0 + step], preferred_element_type=F32)
        for l0 in range(0, step, LANES):
            t0 = (c0 + l0) % d_head
            cos = cos_ref[:, t0:t0 + LANES]
            sin = sin_ref[:, t0:t0 + LANES]
            qr = _rotate(qc[:, l0:l0 + LANES], cos, sin, use_up, 1)
            kr = _rotate(kc[:, l0:l0 + LANES], cos, sin, use_up, 1) * k_scale
            q_ref[:, c0 + l0:c0 + l0 + LANES] = qr.astype(BF16)
            k_ref[:, c0 + l0:c0 + l0 + LANES] = kr.astype(BF16)
    for c0 in range(0, d_v, step):
        vc = jnp.dot(h, w_ref[:, 2 * d_qk + c0:2 * d_qk + c0 + step], preferred_element_type=F32)
        v_ref[:, c0:c0 + step] = vc.astype(BF16)
        g_ref[:, c0:c0 + step] = jnp.dot(h, w_ref[:, 2 * d_qk + d_v + c0:2 * d_qk + d_v + c0 + step],
                                         preferred_element_type=F32)


def _ret_proj(x, nw, w, cos, sin, *, seq, tm):
    T, D = x.shape
    d_qk = D
    d_v = (w.shape[1] - 2 * d_qk) // 2
    d_head = d_qk // R_HEADS
    nseq = seq // tm
    row = lambda i: (i, 0)
    return pl.pallas_call(
        functools.partial(_ret_proj_kernel, d_qk=d_qk, d_v=d_v, d_head=d_head),
        grid=(T // tm,),
        in_specs=[
            pl.BlockSpec((tm, D), row),
            _resident((1, D)),
            _resident(w.shape),
            pl.BlockSpec((tm, d_head), lambda i: (i % nseq, 0)),
            pl.BlockSpec((tm, d_head), lambda i: (i % nseq, 0)),
        ],
        out_specs=[pl.BlockSpec((tm, d_qk), row), pl.BlockSpec((tm, d_qk), row),
                   pl.BlockSpec((tm, d_v), row), pl.BlockSpec((tm, d_v), row)],
        out_shape=[jax.ShapeDtypeStruct((T, d_qk), BF16), jax.ShapeDtypeStruct((T, d_qk), BF16),
                   jax.ShapeDtypeStruct((T, d_v), BF16), jax.ShapeDtypeStruct((T, d_v), F32)],
        compiler_params=_params("parallel"),
        name="ret_proj",
    )(x, nw, w, cos, sin)


def _ret_core_kernel(dec_ref, q_ref, k_ref, v_ref, g_ref, o_ref, rf_ref, rb_ref, cb_ref, dm_ref,
                     *, chunk, rows, nblk):
    t = pl.program_id(2)
    nch = rows // chunk
    lg = -jnp.exp(dec_ref[...])
    lgf = lg[0:1, 0:1]
    lgb = lg[1:2, 0:1]
    pos = lax.broadcasted_iota(jnp.int32, (chunk, 1), 0).astype(F32)
    contract_rows = (((0,), (0,)), ((), ()))
    contract_last = (((1,), (1,)), ((), ()))

    @pl.when(t == 0)
    def _():
        rf_ref[...] = jnp.zeros_like(rf_ref)
        rb_ref[...] = jnp.zeros_like(rb_ref)
        diff = (lax.broadcasted_iota(jnp.int32, (chunk, chunk), 0)
                - lax.broadcasted_iota(jnp.int32, (chunk, chunk), 1)).astype(F32)
        dm_ref[...] = jnp.exp(jnp.where(diff >= 0, diff * lgf, -diff * lgb))

    @pl.when(t < nblk)
    def _():
        blk = nblk - 1 - t
        xi = jnp.exp((chunk - pos) * lgb)
        zeta = jnp.exp(pos * lgb)
        cdecay = jnp.exp(chunk * lgb)
        for c in reversed(range(nch)):
            r0 = c * chunk
            qc = q_ref[r0:r0 + chunk, :]
            kc = k_ref[r0:r0 + chunk, :]
            vc = v_ref[r0:r0 + chunk, :]
            state = rb_ref[...]
            cross = jnp.dot(qc, state.astype(BF16), preferred_element_type=F32) * xi
            cb_ref[pl.ds(pl.multiple_of(blk * rows + r0, chunk), chunk), :] = cross
            kz = (kc.astype(F32) * zeta).astype(BF16)
            rb_ref[...] = state * cdecay + lax.dot_general(kz, vc, contract_rows,
                                                           preferred_element_type=F32)

    @pl.when(t >= nblk)
    def _():
        blk = t - nblk
        xi = jnp.exp((pos + 1.0) * lgf)
        zeta = jnp.exp((chunk - 1.0 - pos) * lgf)
        cdecay = jnp.exp(chunk * lgf)
        for c in range(nch):
            r0 = c * chunk
            qc = q_ref[r0:r0 + chunk, :]
            kc = k_ref[r0:r0 + chunk, :]
            vc = v_ref[r0:r0 + chunk, :]
            s = lax.dot_general(qc, kc, contract_last, preferred_element_type=F32)
            inner = jnp.dot((s * dm_ref[...]).astype(BF16), vc, preferred_element_type=F32)
            state = rf_ref[...]
            cross = jnp.dot(qc, state.astype(BF16), preferred_element_type=F32) * xi
            o = inner + cross + cb_ref[pl.ds(pl.multiple_of(blk * rows + r0, chunk), chunk), :]
            o = _rms(o, NORM_EPS)
            gc = g_ref[r0:r0 + chunk, :]
            o_ref[r0:r0 + chunk, :] = (gc * _sigmoid(gc) * o).astype(BF16)
            kz = (kc.astype(F32) * zeta).astype(BF16)
            rf_ref[...] = state * cdecay + lax.dot_general(kz, vc, contract_rows,
                                                           preferred_element_type=F32)


def _ret_core(dec, q, k, v, g, *, batch, seq, chunk, rows):
    T, d_qk = q.shape
    d_v = v.shape[1]
    dk = d_qk // R_HEADS
    dv = d_v // R_HEADS
    nblk = seq // rows

    def seq_block(b, h, t):
        blk = jnp.where(t < nblk, nblk - 1 - t, t - nblk)
        return (b * nblk + blk, h)

    def fwd_block(b, h, t):
        return (b * nblk + jnp.maximum(t - nblk, 0), h)

    return pl.pallas_call(
        functools.partial(_ret_core_kernel, chunk=chunk, rows=rows, nblk=nblk),
        grid=(batch, R_HEADS, 2 * nblk),
        in_specs=[
            pl.BlockSpec((None, 2, chunk), lambda b, h, t: (h, 0, 0)),
            pl.BlockSpec((rows, dk), seq_block),
            pl.BlockSpec((rows, dk), seq_block),
            pl.BlockSpec((rows, dv), seq_block),
            pl.BlockSpec((rows, dv), fwd_block),
        ],
        out_specs=pl.BlockSpec((rows, dv), fwd_block),
        out_shape=jax.ShapeDtypeStruct((T, d_v), BF16),
        scratch_shapes=[
            pltpu.VMEM((dk, dv), F32),
            pltpu.VMEM((dk, dv), F32),
            pltpu.VMEM((seq, dv), F32),
            pltpu.VMEM((chunk, chunk), F32),
        ],
        compiler_params=_params("parallel", "parallel", "arbitrary"),
        name="ret_core",
    )(dec, q, k, v, g)


def _tail_kernel(x_ref, o_ref, p_ref, wo_ref, n_mix_ref, n_pre_ref, n_post_ref, w1_ref, w2_ref,
                 wg_ref, wp_ref, y_ref):
    m = jnp.dot(o_ref[...], wo_ref[...], preferred_element_type=F32)
    x = x_ref[...] + _rms(m, NORM_EPS) * n_mix_ref[...]
    h = (_rms(x, NORM_EPS) * n_pre_ref[...]).astype(BF16)
    u = jnp.maximum(jnp.dot(h, w1_ref[...], preferred_element_type=F32), 0.0)
    m = jnp.dot((u * u).astype(BF16), w2_ref[...], preferred_element_type=F32)
    x = x + _rms(m, NORM_EPS) * n_post_ref[...]
    gate = _sigmoid(jnp.dot(_rms(x, NORM_EPS).astype(BF16), wg_ref[...], preferred_element_type=F32))
    emb = jnp.dot(p_ref[...].astype(BF16), wp_ref[...], preferred_element_type=F32)
    y_ref[...] = x + emb * gate


def _tail(x, o, p, wo, n_mix, n_pre, n_post, w1, w2, wg, wp, *, tm):
    T, D = x.shape
    row = lambda i: (i, 0)
    return pl.pallas_call(
        _tail_kernel,
        grid=(T // tm,),
        in_specs=[
            pl.BlockSpec((tm, D), row),
            pl.BlockSpec((tm, o.shape[1]), row),
            pl.BlockSpec((tm, p.shape[1]), row),
            _resident(wo.shape), _resident((1, D)), _resident((1, D)), _resident((1, D)),
            _resident(w1.shape), _resident(w2.shape), _resident(wg.shape), _resident(wp.shape),
        ],
        out_specs=pl.BlockSpec((tm, D), row),
        out_shape=jax.ShapeDtypeStruct((T, D), F32),
        compiler_params=_params("parallel"),
        name="layer_tail",
    )(x, o, p, wo, n_mix, n_pre, n_post, w1, w2, wg, wp)


def _rope_tables(seq):
    dim = A_HEAD_DIM
    inv_freq = 1.0 / (ROPE_THETA ** (jnp.arange(0, dim, 2, dtype=F32) / dim))
    ang = jnp.arange(seq, dtype=F32)[:, None] * inv_freq[None, :]
    ang = jnp.concatenate([ang, ang], axis=-1)
    cos, sin = jnp.cos(ang), jnp.sin(ang)
    sign = jnp.where(jnp.arange(dim) < dim // 2, -1.0, 1.0).astype(F32)
    reps = LANES // dim
    return jnp.tile(cos, (1, reps)), jnp.tile(sin * sign, (1, reps))


def _xpos_tables(seq, dim):
    angle = 1.0 / (XPOS_BASE ** jnp.linspace(0.0, 1.0, dim // 2, dtype=F32))
    angle = jnp.repeat(angle, 2)
    ang = jnp.arange(seq, dtype=F32)[:, None] * angle[None, :]
    sign = jnp.where(jnp.arange(dim) % 2 == 0, -1.0, 1.0).astype(F32)
    return jnp.cos(ang), jnp.sin(ang) * sign


def _pick(n, candidates):
    for c in candidates:
        if n % c == 0:
            return c
    raise ValueError(f"no tile for extent {n}")


def _trunk(x, p, norm_pre_mix, norm_post_mix, norm_pre_mlp, norm_post_mlp, attn_w_qkv, attn_w_o,
           attn_lambda_q1, attn_lambda_k1, attn_lambda_q2, attn_lambda_k2, attn_subln, ret_w_in,
           ret_w_out, ret_decay, mlp_w_in, mlp_w_out, ple_w_proj, ple_w_gate):
    batch, seq, d_model = x.shape
    depth = p.shape[0]
    T = batch * seq
    x = x.reshape(T, d_model)
    p = p.reshape(depth, T, p.shape[-1])
    tm = _pick(seq, (512, 256, 128, 64, 32, 16, 8))
    tq = _pick(seq, (256, 128, 64, 32, 16, 8))
    tk = _pick(seq, (512, 256, 128))
    chunk = _pick(seq, (256, 128))
    rows = _pick(seq, (512, 256, 128))
    rope = _rope_tables(seq)
    xpos = _xpos_tables(seq, d_model // R_HEADS)
    vec = lambda a: a.reshape(1, -1)
    for i in range(depth):
        j = i // N_MIXERS
        if i % N_MIXERS == 0:
            wqkv = attn_w_qkv[j].astype(BF16)
            q, k, v = _attn_proj(x, vec(norm_pre_mix[i]), wqkv[:, :2 * d_model],
                                 wqkv[:, 2 * d_model:].T, *rope, seq=seq, tm=tm)
            lam_init = 0.8 - 0.6 * math.exp(-0.3 * i)
            o = _attn_core(q, k, v, vec(attn_lambda_q1[j]), vec(attn_lambda_k1[j]),
                           vec(attn_lambda_q2[j]), vec(attn_lambda_k2[j]), vec(attn_subln[j]),
                           batch=batch, seq=seq, lam_init=lam_init, tq=tq, tk=tk)
            wo = attn_w_o[j]
        else:
            q, k, v, g = _ret_proj(x, vec(norm_pre_mix[i]), ret_w_in[j].astype(BF16), *xpos,
                                   seq=seq, tm=tm)
            dec = jnp.broadcast_to(ret_decay[j].T[:, :, None], (R_HEADS, 2, chunk))
            o = _ret_core(dec, q, k, v, g, batch=batch, seq=seq, chunk=chunk, rows=rows)
            wo = ret_w_out[j]
        x = _tail(x, o, p[i], wo.astype(BF16), vec(norm_post_mix[i]), vec(norm_pre_mlp[i]),
                  vec(norm_post_mlp[i]), mlp_w_in[i].astype(BF16), mlp_w_out[i].astype(BF16),
                  ple_w_gate[i].astype(BF16), ple_w_proj[i].astype(BF16), tm=tm)
    return x.reshape(batch, seq, d_model)


def kernel(x_prompt, x_sample, p_prompt, p_sample, norm_pre_mix, norm_post_mix, norm_pre_mlp, norm_post_mlp, attn_w_qkv, attn_w_o, attn_lambda_q1, attn_lambda_k1, attn_lambda_q2, attn_lambda_k2, attn_subln, ret_w_in, ret_w_out, ret_decay, mlp_w_in, mlp_w_out, ple_w_proj, ple_w_gate):
    weights = (norm_pre_mix, norm_post_mix, norm_pre_mlp, norm_post_mlp, attn_w_qkv, attn_w_o,
               attn_lambda_q1, attn_lambda_k1, attn_lambda_q2, attn_lambda_k2, attn_subln, ret_w_in,
               ret_w_out, ret_decay, mlp_w_in, mlp_w_out, ple_w_proj, ple_w_gate)
    return (_trunk(x_prompt, p_prompt, *weights), _trunk(x_sample, p_sample, *weights))
```
